```python
import functools
import jax
import jax.numpy as jnp
from jax import lax
import numpy as np

D_MODEL = 2048
BATCH = 1
SEQ = 8192
DEPTH = 1
DEC_BATCH = 32
DEC_SEQ = 1
PAST_LEN = 16384
PAGE_SIZE = 128

N_HEADS = 8
HEAD_DIM = 128
D_ATT = N_HEADS * HEAD_DIM
N_IDX_HEADS = 16
IDX_DIM = 64
IDX_WEIGHT_SCALE = (N_IDX_HEADS ** -0.5) * (IDX_DIM ** -0.5)
TOPK_MAX = 256
Q_BLOCK = 128
ROPE_THETA = 10000.0
SGU_GROUPS = 8
SGU_CHUNK = 128
D_SGU = 1024
SGU_GROUP_DIM = D_SGU // SGU_GROUPS
PEER_HEADS = 8
PEER_N_KEYS = 128
PEER_N_EXPERTS = PEER_N_KEYS * PEER_N_KEYS
PEER_QUERY_DIM = 256
PEER_HALF = PEER_QUERY_DIM // 2
PEER_TOPK = 16
PEER_BLOCK = 128
EPS = 1e-6
SPLITS = (D_ATT, D_ATT, D_ATT, N_IDX_HEADS * IDX_DIM, IDX_DIM, N_IDX_HEADS, D_SGU, D_SGU, D_MODEL, D_MODEL)
D_IN = 3 * D_ATT + N_IDX_HEADS * IDX_DIM + IDX_DIM + N_IDX_HEADS + 2 * D_SGU + 2 * D_MODEL

kernel_name = 'dsa_gmlp_peer_hybrid_step'


def rmsnorm(x, g):
    xf = x.astype(jnp.float32)
    y = xf * lax.rsqrt(jnp.mean(xf * xf, axis=-1, keepdims=True) + EPS)
    return (y * g.astype(jnp.float32)).astype(x.dtype)


def rope(x, pos):
    half = x.shape[-1] // 2
    inv = jnp.power(ROPE_THETA, -jnp.arange(half, dtype=jnp.float32) / half)
    ang = pos.astype(jnp.float32)[:, None] * inv[None, :]
    cos = jnp.cos(ang)[:, None, :]
    sin = jnp.sin(ang)[:, None, :]
    xf = x.astype(jnp.float32)
    x1, x2 = xf[..., :half], xf[..., half:]
    return jnp.concatenate([x1 * cos - x2 * sin, x2 * cos + x1 * sin], axis=-1).astype(x.dtype)


def split_cols(z):
    outs = []
    start = 0
    for width in SPLITS:
        outs.append(z[..., start:start + width])
        start += width
    return outs


def gather_rows(rows, idx):
    return jax.vmap(lambda r, i: r[i])(rows, idx)


def project_in(h, pos, w_in):
    B, T, _ = h.shape
    q, k, v, qi, ki, wi, gu, gv, ga, gm = split_cols(h @ w_in)
    q = rope(q.reshape(B, T, N_HEADS, HEAD_DIM), pos)
    k = rope(k.reshape(B, T, N_HEADS, HEAD_DIM), pos)
    v = v.reshape(B, T, N_HEADS, HEAD_DIM)
    qi = rope(qi.reshape(B, T, N_IDX_HEADS, IDX_DIM), pos)
    ki = rope(ki.reshape(B, T, 1, IDX_DIM), pos)[:, :, 0]
    wi = wi * IDX_WEIGHT_SCALE
    return q, k, v, qi, ki, wi, gu, gv, ga, gm


def indexer_scores(qi, wi, ki, q_pos, k_pos):
    dots = jnp.einsum('bqhd,bsd->bqhs', qi, ki, preferred_element_type=jnp.float32)
    score = jnp.einsum('bqhs,bqh->bqs', jax.nn.relu(dots), wi.astype(jnp.float32))
    causal = k_pos[None, None, :] <= q_pos[None, :, None]
    return jnp.where(causal, score, -jnp.inf)


def sparse_attend(q, k_sel, v_sel, valid):
    logits = jnp.einsum('bqhd,bqkhd->bqhk', q, k_sel, preferred_element_type=jnp.float32) * (HEAD_DIM ** -0.5)
    logits = jnp.where(valid[:, :, None, :], logits, -jnp.inf)
    p = jax.nn.softmax(logits, axis=-1).astype(v_sel.dtype)
    return jnp.einsum('bqhk,bqkhd->bqhd', p, v_sel)


def dsa_prompt(q, k, v, qi, ki, wi, topk):
    B, S = q.shape[0], q.shape[1]
    nb = S // Q_BLOCK
    k_pos = jnp.arange(S, dtype=jnp.int32)

    def block(args):
        qb, qib, wib, qpos = args
        scores = indexer_scores(qib, wib, ki, qpos, k_pos)
        _, idx = lax.top_k(scores, topk)
        valid = idx <= qpos[None, :, None]
        return sparse_attend(qb, gather_rows(k, idx), gather_rows(v, idx), valid)

    def to_blocks(a):
        return jnp.moveaxis(a.reshape((B, nb, Q_BLOCK) + a.shape[2:]), 1, 0)

    out = lax.map(block, (to_blocks(q), to_blocks(qi), to_blocks(wi), k_pos.reshape(nb, Q_BLOCK)))
    return jnp.moveaxis(out, 0, 1).reshape(B, S, D_ATT)


def dsa_sample(q, k_new, v_new, qi, ki_new, wi, cache_k, cache_v, cache_idx_k, page_table, topk):
    DB, DS = q.shape[0], q.shape[1]
    past = page_table.shape[1] * PAGE_SIZE
    ki_past = cache_idx_k[page_table].reshape(DB, past, IDX_DIM)
    ki_all = jnp.concatenate([ki_past, ki_new.astype(ki_past.dtype)], axis=1)
    q_pos = past + jnp.arange(DS, dtype=jnp.int32)
    k_pos = jnp.arange(past + DS, dtype=jnp.int32)
    scores = indexer_scores(qi, wi, ki_all, q_pos, k_pos)
    _, idx = lax.top_k(scores, topk)
    past_idx = jnp.minimum(idx, past - 1)
    phys = gather_rows(page_table, past_idx // PAGE_SIZE)
    off = past_idx % PAGE_SIZE
    new_idx = jnp.clip(idx - past, 0, DS - 1)
    is_new = (idx >= past)[..., None, None]
    k_sel = jnp.where(is_new, gather_rows(k_new, new_idx), cache_k[phys, off])
    v_sel = jnp.where(is_new, gather_rows(v_new, new_idx), cache_v[phys, off])
    valid = idx <= q_pos[None, :, None]
    return sparse_attend(q, k_sel, v_sel, valid).reshape(DB, DS, D_ATT)


def spatial_gating(u_raw, v_raw, sgu_g, sgu_w, sgu_b):
    B, L, _ = u_raw.shape
    u = jax.nn.gelu(u_raw)
    v = rmsnorm(jax.nn.gelu(v_raw), sgu_g)
    nc = -(-L // SGU_CHUNK)
    vp = jnp.pad(v, ((0, 0), (0, nc * SGU_CHUNK - L), (0, 0)))
    vc = vp.reshape(B, nc, SGU_CHUNK, SGU_GROUPS, SGU_GROUP_DIM)
    w_mask = sgu_w * jnp.tril(jnp.ones((SGU_CHUNK, SGU_CHUNK), dtype=sgu_w.dtype))[None]
    s = jnp.einsum('gts,bcsgd->bctgd', w_mask, vc) + sgu_b.T[:, :, None]
    s = s.reshape(B, nc * SGU_CHUNK, D_SGU)[:, :L]
    return u * s, v


def peer_block(h, peer_wq, peer_keys, peer_u, peer_v):
    T = h.shape[0]
    q = (h @ peer_wq).reshape(T, PEER_HEADS, 2, PEER_HALF)
    s = jnp.einsum('thcd,hcnd->thcn', q, peer_keys, preferred_element_type=jnp.float32)
    vals, idx = lax.top_k(s, PEER_TOPK)
    cand = (vals[:, :, 0, :, None] + vals[:, :, 1, None, :]).reshape(T, PEER_HEADS, PEER_TOPK * PEER_TOPK)
    cv, ci = lax.top_k(cand, PEER_TOPK)
    i1 = jnp.take_along_axis(idx[:, :, 0], ci // PEER_TOPK, axis=-1)
    i2 = jnp.take_along_axis(idx[:, :, 1], ci % PEER_TOPK, axis=-1)
    expert = i1 * PEER_N_KEYS + i2
    g = jax.nn.softmax(cv, axis=-1)
    act = jax.nn.gelu(jnp.einsum('td,thkd->thk', h, peer_u[expert], preferred_element_type=jnp.float32))
    return jnp.einsum('thk,thkd->td', (g * act).astype(h.dtype), peer_v[expert])


def peer(h, peer_wq, peer_keys, peer_u, peer_v):
    T, D = h.shape
    fn = functools.partial(peer_block, peer_wq=peer_wq, peer_keys=peer_keys, peer_u=peer_u, peer_v=peer_v)
    if T > PEER_BLOCK and T % PEER_BLOCK == 0:
        return lax.map(fn, h.reshape(T // PEER_BLOCK, PEER_BLOCK, D)).reshape(T, D)
    return fn(h)


def decoder_layer(x, pos, attend, norm1_g, w_in, w_a, w_m, w_o, sgu_g, sgu_w, sgu_b,
                  norm2_g, peer_wq, peer_keys, peer_u, peer_v):
    h = rmsnorm(x, norm1_g)
    q, k, v, qi, ki, wi, gu, gv, ga, gm = project_in(h, pos, w_in)
    a = attend(q, k, v, qi, ki, wi)
    m, v_sgu = spatial_gating(gu, gv, sgu_g, sgu_w, sgu_b)
    merged = jax.nn.sigmoid(ga) * (a @ w_a) + jax.nn.sigmoid(gm) * (m @ w_m)
    x = x + merged @ w_o
    B, T, D = x.shape
    h2 = rmsnorm(x, norm2_g).reshape(B * T, D)
    x = x + peer(h2, peer_wq, peer_keys, peer_u, peer_v).reshape(B, T, D)
    return x, k, v, ki, v_sgu


def setup_inputs(seed: int = 0):
    key = jax.random.key(seed)
    ks = jax.random.split(key, 20)
    f32 = jnp.float32
    n_pages = PAST_LEN // PAGE_SIZE
    n_used = DEC_BATCH * n_pages
    n_phys = n_used + n_used // 4
    page_table = jax.random.permutation(ks[0], n_phys)[:n_used].reshape(DEC_BATCH, n_pages).astype(jnp.int32)

    def nrm(k, shape, scale):
        return jax.random.normal(k, shape, f32) * scale

    return {
        'x_prompt': nrm(ks[1], (BATCH, SEQ, D_MODEL), 1.0),
        'x_sample': nrm(ks[2], (DEC_BATCH, DEC_SEQ, D_MODEL), 1.0),
        'cache_k': nrm(ks[3], (n_phys, PAGE_SIZE, N_HEADS, HEAD_DIM), 1.0),
        'cache_v': nrm(ks[4], (n_phys, PAGE_SIZE, N_HEADS, HEAD_DIM), 1.0),
        'cache_idx_k': nrm(ks[5], (n_phys, PAGE_SIZE, IDX_DIM), 1.0),
        'page_table': page_table,
        'norm1_g': 1.0 + nrm(ks[6], (D_MODEL,), 0.02),
        'w_in': nrm(ks[7], (D_MODEL, D_IN), D_MODEL ** -0.5),
        'w_a': nrm(ks[8], (D_ATT, D_MODEL), D_ATT ** -0.5),
        'w_m': nrm(ks[9], (D_SGU, D_MODEL), D_SGU ** -0.5),
        'w_o': nrm(ks[10], (D_MODEL, D_MODEL), D_MODEL ** -0.5),
        'sgu_g': 1.0 + nrm(ks[11], (D_SGU,), 0.02),
        'sgu_w': nrm(ks[12], (SGU_GROUPS, SGU_CHUNK, SGU_CHUNK), SGU_CHUNK ** -0.5),
        'sgu_b': 1.0 + nrm(ks[13], (SGU_GROUPS, SGU_CHUNK), 0.1),
        'norm2_g': 1.0 + nrm(ks[14], (D_MODEL,), 0.02),
        'peer_wq': nrm(ks[15], (D_MODEL, PEER_HEADS * PEER_QUERY_DIM), D_MODEL ** -0.5),
        'peer_keys': nrm(ks[16], (PEER_HEADS, 2, PEER_N_KEYS, PEER_HALF), PEER_HALF ** -0.5),
        'peer_u': nrm(ks[17], (PEER_N_EXPERTS, D_MODEL), D_MODEL ** -0.5),
        'peer_v': nrm(ks[18], (PEER_N_EXPERTS, D_MODEL), 0.5),
        'norm_f_g': 1.0 + nrm(ks[19], (D_MODEL,), 0.02),
    }


def reference(x_prompt, x_sample, cache_k, cache_v, cache_idx_k, page_table, norm1_g, w_in, w_a, w_m, w_o,
              sgu_g, sgu_w, sgu_b, norm2_g, peer_wq, peer_keys, peer_u, peer_v, norm_f_g):
    seq = x_prompt.shape[1]
    dec_seq = x_sample.shape[1]
    past = page_table.shape[1] * PAGE_SIZE
    topk_prompt = min(TOPK_MAX, seq // 4)
    topk_sample = min(TOPK_MAX, (past + dec_seq) // 4)
    pos_prompt = jnp.arange(seq, dtype=jnp.int32)
    pos_sample = past + jnp.arange(dec_seq, dtype=jnp.int32)

    attend_prompt = functools.partial(dsa_prompt, topk=topk_prompt)

    def attend_sample(q, k, v, qi, ki, wi):
        return dsa_sample(q, k, v, qi, ki, wi, cache_k, cache_v, cache_idx_k, page_table, topk_sample)

    xp, xs = x_prompt, x_sample
    for _ in range(DEPTH):
        xp, k_p, v_p, ki_p, _v_sgu_p = decoder_layer(
            xp, pos_prompt, attend_prompt, norm1_g, w_in, w_a, w_m, w_o, sgu_g, sgu_w, sgu_b,
            norm2_g, peer_wq, peer_keys, peer_u, peer_v)
        xs, k_s, v_s, ki_s, v_sgu_s = decoder_layer(
            xs, pos_sample, attend_sample, norm1_g, w_in, w_a, w_m, w_o, sgu_g, sgu_w, sgu_b,
            norm2_g, peer_wq, peer_keys, peer_u, peer_v)
    y_prompt = rmsnorm(xp, norm_f_g)
    y_sample = rmsnorm(xs, norm_f_g)
    return (y_prompt, y_sample, k_p, v_p, ki_p, k_s, v_s, ki_s, v_sgu_s)
```

```python
import functools

import jax
import jax.numpy as jnp
from jax import lax
from jax.experimental import pallas as pl
from jax.experimental.pallas import tpu as pltpu

F32, BF16, I32 = jnp.float32, jnp.bfloat16, jnp.int32

N_HEADS = 8
HEAD_DIM = 128
D_ATT = N_HEADS * HEAD_DIM
N_IDX_HEADS = 16
IDX_DIM = 64
IDX_WEIGHT_SCALE = (N_IDX_HEADS ** -0.5) * (IDX_DIM ** -0.5)
TOPK_MAX = 256
ROPE_THETA = 10000.0
PAGE_SIZE = 128
SGU_GROUPS = 8
SGU_CHUNK = 128
D_SGU = 1024
PEER_HEADS = 8
PEER_N_KEYS = 128
PEER_TOPK = 16
EPS = 1e-6

LANES = 128
SUBLANES = 8
VMEM_LIMIT = 56 * 1024 * 1024
INT_MIN = -2 ** 31
NEG_BIG = -1e30


def _cparams(*sem):
    return pltpu.CompilerParams(dimension_semantics=sem, vmem_limit_bytes=VMEM_LIMIT)


def _full(shape):
    n = len(shape)
    return pl.BlockSpec(shape, lambda *_: (0,) * n, pipeline_mode=pl.Buffered(1))


def _rows(tm, width):
    return pl.BlockSpec((tm, width), lambda i: (i, 0))


def _dot(a, b):
    return jnp.dot(a, b, preferred_element_type=F32)


def _dot_nt(a, b):
    return lax.dot_general(a, b, (((1,), (1,)), ((), ())), preferred_element_type=F32)


def _rms(x):
    return x * lax.rsqrt(jnp.mean(x * x, axis=-1, keepdims=True) + EPS)


def _rmsnorm_kernel(x_ref, g_ref, o_ref):
    o_ref[...] = (_rms(x_ref[...]) * g_ref[...]).astype(o_ref.dtype)


def _rmsnorm(x, g, tm, out_dtype):
    r, d = x.shape
    return pl.pallas_call(
        _rmsnorm_kernel,
        out_shape=jax.ShapeDtypeStruct((r, d), out_dtype),
        grid=(r // tm,),
        in_specs=[_rows(tm, d), _full((1, d))],
        out_specs=_rows(tm, d),
        compiler_params=_cparams("parallel"),
        name="rmsnorm",
    )(x, g.reshape(1, d))


def _add_rmsnorm_kernel(x_ref, y_ref, g_ref, o_ref):
    o_ref[...] = _rms(x_ref[...] + y_ref[...]) * g_ref[...]


def _add_rmsnorm(x, y, g, tm):
    r, d = x.shape
    return pl.pallas_call(
        _add_rmsnorm_kernel,
        out_shape=jax.ShapeDtypeStruct((r, d), F32),
        grid=(r // tm,),
        in_specs=[_rows(tm, d), _rows(tm, d), _full((1, d))],
        out_specs=_rows(tm, d),
        compiler_params=_cparams("parallel"),
        name="add_rmsnorm",
    )(x, y, g.reshape(1, d))


def _rope_tables(pos, half, width):
    inv = jnp.power(ROPE_THETA, -jnp.arange(half, dtype=F32) / half)
    ang = pos.astype(F32)[:, None] * inv[None, :]
    cos, sin = jnp.cos(ang), jnp.sin(ang)
    zero = jnp.zeros_like(sin)
    reps = width // (2 * half)
    c = jnp.tile(jnp.concatenate([cos, cos], 1), (1, reps))
    a = jnp.tile(jnp.concatenate([-sin, zero], 1), (1, reps))
    b = jnp.tile(jnp.concatenate([zero, sin], 1), (1, reps))
    return c, a, b


def _rope(z, c, a, b, half):
    return z * c + pltpu.roll(z, LANES - half, 1) * a + pltpu.roll(z, half, 1) * b


def _proj_qk_kernel(h_ref, w_ref, c_ref, a_ref, b_ref, q_ref, kf_ref, kb_ref):
    h = h_ref[...]
    c, a, b = c_ref[...], a_ref[...], b_ref[...]
    for j in range(2 * D_ATT // 256):
        z = _dot(h, w_ref[:, j * 256:(j + 1) * 256])
        for s in range(2):
            col = j * 256 + s * LANES
            r = _rope(z[:, s * LANES:(s + 1) * LANES], c, a, b, HEAD_DIM // 2)
            if col < D_ATT:
                q_ref[:, col:col + LANES] = r.astype(BF16)
            else:
                kf_ref[:, col - D_ATT:col - D_ATT + LANES] = r
                kb_ref[:, col - D_ATT:col - D_ATT + LANES] = r.astype(BF16)


def _proj_qk(h, w, tabs, tm):
    r, d = h.shape
    return pl.pallas_call(
        _proj_qk_kernel,
        out_shape=(jax.ShapeDtypeStruct((r, D_ATT), BF16),
                   jax.ShapeDtypeStruct((r, D_ATT), F32),
                   jax.ShapeDtypeStruct((r, D_ATT), BF16)),
        grid=(r // tm,),
        in_specs=[_rows(tm, d), _full(w.shape)] + [_rows(tm, LANES)] * 3,
        out_specs=(_rows(tm, D_ATT),) * 3,
        compiler_params=_cparams("parallel"),
        name="proj_qk",
    )(h, w, *tabs)


def _proj_vqi_kernel(h_ref, w_ref, c_ref, a_ref, b_ref, vf_ref, vb_ref, qi_ref):
    h = h_ref[...]
    c, a, b = c_ref[...], a_ref[...], b_ref[...]
    for j in range(2 * D_ATT // 256):
        z = _dot(h, w_ref[:, j * 256:(j + 1) * 256])
        col = j * 256
        if col < D_ATT:
            vf_ref[:, col:col + 256] = z
            vb_ref[:, col:col + 256] = z.astype(BF16)
        else:
            for s in range(2):
                r = _rope(z[:, s * LANES:(s + 1) * LANES], c, a, b, IDX_DIM // 2)
                o = col - D_ATT + s * LANES
                qi_ref[:, o:o + LANES] = r.astype(BF16)


def _proj_vqi(h, w, tabs, tm):
    r, d = h.shape
    return pl.pallas_call(
        _proj_vqi_kernel,
        out_shape=(jax.ShapeDtypeStruct((r, D_ATT), F32),
                   jax.ShapeDtypeStruct((r, D_ATT), BF16),
                   jax.ShapeDtypeStruct((r, N_IDX_HEADS * IDX_DIM), BF16)),
        grid=(r // tm,),
        in_specs=[_rows(tm, d), _full(w.shape)] + [_rows(tm, LANES)] * 3,
        out_specs=(_rows(tm, D_ATT),) * 3,
        compiler_params=_cparams("parallel"),
        name="proj_vqi",
    )(h, w, *tabs)


def _proj_kiw_kernel(h_ref, w_ref, c_ref, a_ref, b_ref, o_ref):
    z = _dot(h_ref[...], w_ref[...])
    o_ref[...] = _rope(z, c_ref[...], a_ref[...], b_ref[...], IDX_DIM // 2)


def _proj_kiw(h, w, tabs, tm):
    r, d = h.shape
    return pl.pallas_call(
        _proj_kiw_kernel,
        out_shape=jax.ShapeDtypeStruct((r, LANES), F32),
        grid=(r // tm,),
        in_specs=[_rows(tm, d), _full(w.shape)] + [_rows(tm, LANES)] * 3,
        out_specs=_rows(tm, LANES),
        compiler_params=_cparams("parallel"),
        name="proj_kiw",
    )(h, w, *tabs)


def _sgu_kernel(h_ref, wu_ref, wv_ref, g_ref, ws_ref, bs_ref, m_ref, vn_ref, *, single_step):
    h = h_ref[...]
    tm = h.shape[0]
    u = jax.nn.gelu(_dot(h, wu_ref[...]))
    vn = _rms(jax.nn.gelu(_dot(h, wv_ref[...]))) * g_ref[...]
    vn_ref[...] = vn
    vb = vn.astype(BF16)
    if single_step:
        s = ws_ref[...].astype(F32) * vb.astype(F32) + bs_ref[...]
        m_ref[...] = (u * s).astype(BF16)
    else:
        tril = (lax.broadcasted_iota(I32, (SGU_CHUNK, SGU_CHUNK), 1)
                <= lax.broadcasted_iota(I32, (SGU_CHUNK, SGU_CHUNK), 0))
        for g in range(SGU_GROUPS):
            wg = jnp.where(tril, ws_ref[g], 0.0).astype(BF16)
            lo = g * LANES
            for c in range(tm // SGU_CHUNK):
                r0 = c * SGU_CHUNK
                s = _dot(wg, vb[r0:r0 + SGU_CHUNK, lo:lo + LANES]) + bs_ref[:, lo:lo + LANES]
                m_ref[r0:r0 + SGU_CHUNK, lo:lo + LANES] = (u[r0:r0 + SGU_CHUNK, lo:lo + LANES] * s).astype(BF16)


def _sgu(h, wu, wv, sgu_g, sgu_w, sgu_b, tm, single_step):
    r, d = h.shape
    if single_step:
        ws = jnp.repeat(sgu_w[:, 0, 0], LANES).reshape(1, D_SGU).astype(BF16)
        bs = jnp.repeat(sgu_b[:, 0], LANES).reshape(1, D_SGU)
    else:
        ws = sgu_w
        bs = jnp.repeat(sgu_b.T, LANES, axis=1)
    return pl.pallas_call(
        functools.partial(_sgu_kernel, single_step=single_step),
        out_shape=(jax.ShapeDtypeStruct((r, D_SGU), BF16), jax.ShapeDtypeStruct((r, D_SGU), F32)),
        grid=(r // tm,),
        in_specs=[_rows(tm, d), _full(wu.shape), _full(wv.shape), _full((1, D_SGU)),
                  _full(ws.shape), _full(bs.shape)],
        out_specs=(_rows(tm, D_SGU), _rows(tm, D_SGU)),
        compiler_params=_cparams("parallel"),
        name="sgu",
    )(h, wu, wv, sgu_g.reshape(1, D_SGU), ws, bs)


def _dsa_prompt_kernel(q_ref, qi_ref, kiw_ref, k_ref, v_ref, kia_ref, kib_ref, o_ref,
                       key_scr, wb_scr, m_scr, l_scr, acc_scr, *, tq, topk):
    i = pl.program_id(0)
    nchunk = i + 1
    reps = tq // LANES

    kiw = kiw_ref[...]
    for h in range(N_IDX_HEADS):
        wb_scr[h] = jnp.broadcast_to(kiw[:, IDX_DIM + h:IDX_DIM + h + 1], (tq, LANES))

    row = i * tq + lax.broadcasted_iota(I32, (tq, tq), 0)
    col = lax.broadcasted_iota(I32, (tq, tq), 1)

    def chunk_start(j):
        return pl.multiple_of(j * tq, tq)

    def score_chunk(j, carry):
        off = chunk_start(j)
        ka = kia_ref[pl.ds(off, tq), :]
        kb = kib_ref[pl.ds(off, tq), :]
        sc = jnp.zeros((tq, tq), F32)
        for h in range(N_IDX_HEADS):
            p = h // 2
            d = _dot_nt(qi_ref[:, p * LANES:(p + 1) * LANES], ka if h % 2 == 0 else kb)
            w = wb_scr[h]
            sc = sc + jnp.concatenate([w] * reps, axis=1) * jnp.maximum(d, 0.0)
        bits = pltpu.bitcast(sc, I32)
        key = jnp.where(bits < 0, bits ^ 0x7FFFFFFF, bits)
        key = jnp.where(j * tq + col <= row, key, INT_MIN)
        key_scr[:, pl.ds(off, tq)] = key
        return carry

    lax.fori_loop(0, nchunk, score_chunk, 0)

    def count_ge(cand):
        def body(j, acc):
            ge = (key_scr[:, pl.ds(chunk_start(j), tq)] >= cand).astype(I32)
            for r in range(reps):
                acc = acc + ge[:, r * LANES:(r + 1) * LANES]
            return acc
        acc = lax.fori_loop(0, nchunk, body, jnp.zeros((tq, LANES), I32))
        return jnp.sum(acc, axis=1, keepdims=True)

    def bit_step(b, t):
        cand = t ^ jnp.left_shift(jnp.int32(1), 31 - b)
        return jnp.where(count_ge(cand) >= topk, cand, t)

    thr = lax.fori_loop(0, 32, bit_step, jnp.full((tq, 1), INT_MIN, I32))

    m_scr[...] = jnp.full(m_scr.shape, NEG_BIG, F32)
    l_scr[...] = jnp.zeros(l_scr.shape, F32)
    acc_scr[...] = jnp.zeros(acc_scr.shape, F32)
    scale = HEAD_DIM ** -0.5

    def attend_chunk(j, carry):
        off = chunk_start(j)
        kc = key_scr[:, pl.ds(off, tq)]
        bias = jnp.where((kc >= thr) & (kc > INT_MIN), 0.0, NEG_BIG)
        for h in range(N_HEADS):
            lo = h * HEAD_DIM
            s = _dot_nt(q_ref[:, lo:lo + HEAD_DIM], k_ref[pl.ds(off, tq), lo:lo + HEAD_DIM]) * scale + bias
            m_old = m_scr[h][:, :1]
            m_new = jnp.maximum(m_old, jnp.max(s, axis=1, keepdims=True))
            alpha = jnp.exp(m_old - m_new)
            p = jnp.exp(s - m_new)
            l_new = alpha * l_scr[h][:, :1] + jnp.sum(p, axis=1, keepdims=True)
            pv = _dot(p.astype(BF16), v_ref[pl.ds(off, tq), lo:lo + HEAD_DIM])
            acc_scr[h] = alpha * acc_scr[h] + pv
            m_scr[h] = jnp.broadcast_to(m_new, (tq, LANES))
            l_scr[h] = jnp.broadcast_to(l_new, (tq, LANES))
        return carry

    lax.fori_loop(0, nchunk, attend_chunk, 0)

    for h in range(N_HEADS):
        lo = h * HEAD_DIM
        o_ref[:, lo:lo + HEAD_DIM] = (acc_scr[h] / l_scr[h][:, :1]).astype(o_ref.dtype)


def _dsa_prompt(q, qi, kiw, k, v, kia, kib, tq, topk):
    t = q.shape[0]
    return pl.pallas_call(
        functools.partial(_dsa_prompt_kernel, tq=tq, topk=topk),
        out_shape=jax.ShapeDtypeStruct((t, D_ATT), BF16),
        grid=(t // tq,),
        in_specs=[_rows(tq, D_ATT), _rows(tq, N_IDX_HEADS * IDX_DIM), _rows(tq, LANES),
                  _full(k.shape), _full(v.shape), _full(kia.shape), _full(kib.shape)],
        out_specs=_rows(tq, D_ATT),
        scratch_shapes=[pltpu.VMEM((tq, t), I32),
                        pltpu.VMEM((N_IDX_HEADS, tq, LANES), F32),
                        pltpu.VMEM((N_HEADS, tq, LANES), F32),
                        pltpu.VMEM((N_HEADS, tq, LANES), F32),
                        pltpu.VMEM((N_HEADS, tq, HEAD_DIM), F32)],
        compiler_params=_cparams("arbitrary"),
        name="dsa_prompt",
    )(q, qi, kiw, k, v, kia, kib)


def _merge_kernel(h_ref, a_ref, m_ref, wga_ref, wgm_ref, wa_ref, wm_ref, o_ref):
    h = h_ref[...]
    ga = jax.nn.sigmoid(_dot(h, wga_ref[...]))
    gm = jax.nn.sigmoid(_dot(h, wgm_ref[...]))
    o_ref[...] = (ga * _dot(a_ref[...], wa_ref[...]) + gm * _dot(m_ref[...], wm_ref[...])).astype(o_ref.dtype)


def _merge(h, a, m, wga, wgm, wa, wm, tm, tn):
    r, d = h.shape
    n = wga.shape[1]
    col = lambda k: pl.BlockSpec((k, tn), lambda i, j: (0, j))
    row = lambda k: pl.BlockSpec((tm, k), lambda i, j: (i, 0))
    return pl.pallas_call(
        _merge_kernel,
        out_shape=jax.ShapeDtypeStruct((r, n), BF16),
        grid=(r // tm, n // tn),
        in_specs=[row(d), row(D_ATT), row(D_SGU), col(d), col(d), col(D_ATT), col(D_SGU)],
        out_specs=pl.BlockSpec((tm, tn), lambda i, j: (i, j)),
        compiler_params=_cparams("parallel", "arbitrary"),
        name="merge",
    )(h, a, m, wga, wgm, wa, wm)


def _outproj_kernel(x_ref, mg_ref, wo_ref, g_ref, x1_ref, h2_ref):
    x1 = x_ref[...] + _dot(mg_ref[...], wo_ref[...])
    x1_ref[...] = x1
    h2_ref[...] = (_rms(x1) * g_ref[...]).astype(BF16)


def _outproj(x, mg, wo, g2, tm):
    r, d = x.shape
    return pl.pallas_call(
        _outproj_kernel,
        out_shape=(jax.ShapeDtypeStruct((r, d), F32), jax.ShapeDtypeStruct((r, d), BF16)),
        grid=(r // tm,),
        in_specs=[_rows(tm, d), _rows(tm, d), _full(wo.shape), _full((1, d))],
        out_specs=(_rows(tm, d), _rows(tm, d)),
        compiler_params=_cparams("parallel"),
        name="outproj",
    )(x, mg, wo, g2.reshape(1, d))


def _top_rows(x, k):
    vals = []
    for _ in range(k):
        m = jnp.max(x, axis=0, keepdims=True)
        vals.append(m)
        x = jnp.where(x >= m, -jnp.inf, x)
    return vals


def _peer_route_kernel(h2_ref, wq_ref, keys_ref, s_ref, e1_ref, e2_ref, thr_ref):
    qp = _dot(h2_ref[...], wq_ref[...]).astype(BF16)
    s1 = _dot_nt(keys_ref[0], qp[:, :LANES])
    s2 = _dot_nt(keys_ref[1], qp[:, LANES:])
    s_ref[0] = s1
    s_ref[1] = s2
    v1 = _top_rows(s1, PEER_TOPK)
    v2 = _top_rows(s2, PEER_TOPK)
    v2s = jnp.concatenate(v2, axis=0)
    cand = jnp.concatenate([v2s + v1[a] for a in range(PEER_TOPK)], axis=0)
    cv = _top_rows(cand, PEER_TOPK)
    z = jnp.ones_like(cv[0])
    for kk in range(1, PEER_TOPK):
        z = z + jnp.exp(cv[kk] - cv[0])
    e1_ref[0] = jnp.exp(s1 - v1[0]) / z
    e2_ref[0] = jnp.exp(s2 - v2[0])
    thr_ref[0] = jnp.broadcast_to(cv[PEER_TOPK - 1], thr_ref.shape[1:])


def _peer_route(h2, wq, keys, tt):
    r, d = h2.shape
    nk = PEER_N_KEYS
    tok3 = lambda lead, mid: pl.BlockSpec((lead, mid, tt), lambda i, h: (h, 0, i))
    return pl.pallas_call(
        _peer_route_kernel,
        out_shape=(jax.ShapeDtypeStruct((2 * PEER_HEADS, nk, r), F32),
                   jax.ShapeDtypeStruct((PEER_HEADS, nk, r), F32),
                   jax.ShapeDtypeStruct((PEER_HEADS, nk, r), F32),
                   jax.ShapeDtypeStruct((PEER_HEADS, SUBLANES, r), F32)),
        grid=(r // tt, PEER_HEADS),
        in_specs=[pl.BlockSpec((tt, d), lambda i, h: (i, 0)),
                  pl.BlockSpec((d, 2 * LANES), lambda i, h: (0, h)),
                  pl.BlockSpec((2, nk, LANES), lambda i, h: (h, 0, 0))],
        out_specs=(tok3(2, nk), tok3(1, nk), tok3(1, nk), tok3(1, SUBLANES)),
        compiler_params=_cparams("parallel", "arbitrary"),
        name="peer_route",
    )(h2, wq, keys)


def _peer_dense_kernel(h2_ref, s_ref, e1_ref, e2_ref, thr_ref, u_ref, v_ref, o_ref, *, ni1):
    j = pl.program_id(1)

    @pl.when(j == 0)
    def _():
        o_ref[...] = jnp.zeros(o_ref.shape, F32)

    nk = PEER_N_KEYS
    tt = h2_ref.shape[0]
    act = jax.nn.gelu(_dot_nt(u_ref[...], h2_ref[...]))
    parts = []
    for r in range(ni1):
        i1 = j * ni1 + r
        w = jnp.zeros((nk, tt), F32)
        for h in range(PEER_HEADS):
            s1 = s_ref[2 * h, pl.ds(i1, 1), :]
            e1 = e1_ref[h, pl.ds(i1, 1), :]
            sel = (s_ref[2 * h + 1] + s1) >= thr_ref[h, 0:1, :]
            w = w + jnp.where(sel, e2_ref[h] * e1, 0.0)
        parts.append(w * act[r * nk:(r + 1) * nk, :])
    gt = jnp.concatenate(parts, axis=0).T.astype(BF16)
    o_ref[...] += _dot(gt, v_ref[...])


def _peer_dense(h2, s, e1, e2, thr, pu, pv, tt, ni1):
    r, d = h2.shape
    nk = PEER_N_KEYS
    ne = pu.shape[0]
    ec = ni1 * nk
    tok2 = pl.BlockSpec((tt, d), lambda i, j: (i, 0))
    tok3 = lambda lead, mid: pl.BlockSpec((lead, mid, tt), lambda i, j: (0, 0, i), pipeline_mode=pl.Buffered(1))
    ex = pl.BlockSpec((ec, d), lambda i, j: (j, 0))
    return pl.pallas_call(
        functools.partial(_peer_dense_kernel, ni1=ni1),
        out_shape=jax.ShapeDtypeStruct((r, d), F32),
        grid=(r // tt, ne // ec),
        in_specs=[pl.BlockSpec((tt, d), lambda i, j: (i, 0), pipeline_mode=pl.Buffered(1)),
                  tok3(2 * PEER_HEADS, nk), tok3(PEER_HEADS, nk), tok3(PEER_HEADS, nk),
                  tok3(PEER_HEADS, SUBLANES), ex, ex],
        out_specs=tok2,
        compiler_params=_cparams("parallel", "arbitrary"),
        name="peer_dense",
    )(h2, s, e1, e2, thr, pu, pv)


def _idx_scores_kernel(pt_ref, lhs_ref, w_ref, kin_ref, cache_ref, o_ref, on_ref, buf, sem, *, n_pages):
    b = pl.program_id(0)
    rows = PAGE_SIZE // 2

    def page_copy(p):
        return pltpu.make_async_copy(cache_ref.at[pt_ref[b, p]], buf.at[pl.ds(p * rows, rows)], sem)

    def start(p, c):
        page_copy(p).start()
        return c

    def wait(p, c):
        page_copy(p).wait()
        return c

    lax.fori_loop(0, n_pages, start, 0)
    lax.fori_loop(0, n_pages, wait, 0)
    lhs = lhs_ref[0]
    w = w_ref[0][:, :1]
    wr = w * jnp.maximum(_dot_nt(lhs, buf[...].astype(BF16)), 0.0)
    o_ref[0, 0:1, :] = jnp.sum(wr[:N_IDX_HEADS], axis=0, keepdims=True)
    o_ref[0, 1:2, :] = jnp.sum(wr[N_IDX_HEADS:], axis=0, keepdims=True)
    dn = jnp.sum(lhs.astype(F32) * kin_ref[0].astype(F32), axis=1, keepdims=True)
    wn = w * jnp.maximum(dn, 0.0)
    on_ref[0] = jnp.broadcast_to(jnp.sum(wn[:N_IDX_HEADS], axis=0, keepdims=True), on_ref.shape[1:])


def _idx_scores(page_table, lhs, wcol, kin, cache2):
    db, n_pages = page_table.shape
    half = n_pages * PAGE_SIZE // 2
    per_b = lambda s1, s2: pl.BlockSpec((1, s1, s2), lambda b, pt: (b, 0, 0))
    grid_spec = pltpu.PrefetchScalarGridSpec(
        num_scalar_prefetch=1,
        grid=(db,),
        in_specs=[per_b(2 * N_IDX_HEADS, LANES), per_b(2 * N_IDX_HEADS, LANES), per_b(1, LANES),
                  pl.BlockSpec(memory_space=pl.ANY)],
        out_specs=(per_b(2, half), per_b(SUBLANES, LANES)),
        scratch_shapes=[pltpu.VMEM((half, LANES), F32), pltpu.SemaphoreType.DMA],
    )
    return pl.pallas_call(
        functools.partial(_idx_scores_kernel, n_pages=n_pages),
        out_shape=(jax.ShapeDtypeStruct((db, 2, half), F32), jax.ShapeDtypeStruct((db, SUBLANES, LANES), F32)),
        grid_spec=grid_spec,
        compiler_params=_cparams("arbitrary"),
        name="idx_scores",
    )(page_table, lhs, wcol, kin, cache2)


def _topk_kernel(s_ref, sn_ref, pos_ref, idx_ref, scr, *, topk, past):
    scr[...] = s_ref[...]
    pos = pos_ref[...]
    rows = s_ref.shape[0]
    lane = lax.broadcasted_iota(I32, (rows, topk), 1)

    def body(k, carry):
        idx, sn = carry
        s = scr[...]
        m = jnp.maximum(jnp.max(s, axis=1, keepdims=True), sn)
        at = jnp.min(jnp.where(s == m, pos, past), axis=1, keepdims=True)
        scr[...] = jnp.where(pos == at, -jnp.inf, s)
        sn = jnp.where(at == past, -jnp.inf, sn)
        return jnp.where(lane == k, at, idx), sn

    idx, _ = lax.fori_loop(0, topk, body, (jnp.zeros((rows, topk), I32), sn_ref[...]))
    idx_ref[...] = idx


def _topk_rows(s, sn, pos, topk, past):
    rows = s.shape[0]
    return pl.pallas_call(
        functools.partial(_topk_kernel, topk=topk, past=past),
        out_shape=jax.ShapeDtypeStruct((rows, topk), I32),
        grid=(1,),
        in_specs=[_full(s.shape), _full(sn.shape), _full(pos.shape)],
        out_specs=_full((rows, topk)),
        scratch_shapes=[pltpu.VMEM(s.shape, F32)],
        compiler_params=_cparams("arbitrary"),
        name="topk_rows",
    )(s, sn, pos)


def _gather_attend_kernel(idx_ref, pt_ref, q_ref, kn_ref, vn_ref, ck_ref, cv_ref, o_ref,
                          kbuf, vbuf, ixbuf, sem, *, topk, past, q_pos):
    b = pl.program_id(0)

    def copies(t):
        old = jnp.minimum(idx_ref[b, t], past - 1)
        phys = pt_ref[b, old // PAGE_SIZE]
        off = old % PAGE_SIZE
        return (pltpu.make_async_copy(ck_ref.at[phys, off], kbuf.at[t], sem.at[0]),
                pltpu.make_async_copy(cv_ref.at[phys, off], vbuf.at[t], sem.at[1]))

    def start(t, c):
        ck, cv = copies(t)
        ck.start()
        cv.start()
        ixbuf[t] = jnp.full(ixbuf.shape[1:], idx_ref[b, t], I32)
        return c

    def wait(t, c):
        ck, cv = copies(t)
        ck.wait()
        cv.wait()
        return c

    lax.fori_loop(0, topk, start, 0)
    lax.fori_loop(0, topk, wait, 0)

    ix = ixbuf[...]
    k_sel = jnp.where(ix >= past, kn_ref[...], kbuf[...])
    v_sel = jnp.where(ix >= past, vn_ref[...], vbuf[...])
    logits = jnp.sum(k_sel * q_ref[...], axis=-1, keepdims=True) * (HEAD_DIM ** -0.5)
    logits = jnp.where(ix[:, :, :1] <= q_pos, logits, -jnp.inf)
    p = jnp.exp(logits - jnp.max(logits, axis=0, keepdims=True))
    p = p / jnp.sum(p, axis=0, keepdims=True)
    o_ref[...] = jnp.sum(p * v_sel, axis=0, keepdims=True)


def _gather_attend(idx, page_table, q, kn, vn, cache_k, cache_v, past):
    db, topk = idx.shape
    per_b = pl.BlockSpec((1, N_HEADS, HEAD_DIM), lambda b, ix, pt: (b, 0, 0))
    grid_spec = pltpu.PrefetchScalarGridSpec(
        num_scalar_prefetch=2,
        grid=(db,),
        in_specs=[per_b, per_b, per_b, pl.BlockSpec(memory_space=pl.ANY), pl.BlockSpec(memory_space=pl.ANY)],
        out_specs=per_b,
        scratch_shapes=[pltpu.VMEM((topk, N_HEADS, HEAD_DIM), F32),
                        pltpu.VMEM((topk, N_HEADS, HEAD_DIM), F32),
                        pltpu.VMEM((topk, N_HEADS, HEAD_DIM), I32),
                        pltpu.SemaphoreType.DMA((2,))],
    )
    return pl.pallas_call(
        functools.partial(_gather_attend_kernel, topk=topk, past=past, q_pos=past),
        out_shape=jax.ShapeDtypeStruct((db, N_HEADS, HEAD_DIM), F32),
        grid_spec=grid_spec,
        compiler_params=_cparams("arbitrary"),
        name="gather_attend",
    )(idx, page_table, q, kn, vn, cache_k, cache_v)


def _split_w_in(w_in):
    wb = w_in.astype(BF16)
    o = 0
    parts = []
    for width in (2 * D_ATT, 2 * D_ATT, IDX_DIM + N_IDX_HEADS, D_SGU, D_SGU, 2 * 1024, 2 * 1024):
        parts.append(wb[:, o:o + width])
        o += width
    w_qk, w_vqi, w_kiw, w_gu, w_gv, w_ga, w_gm = parts
    w_kiw = jnp.pad(w_kiw, ((0, 0), (0, LANES - w_kiw.shape[1])))
    return w_qk, w_vqi, w_kiw, w_gu, w_gv, w_ga, w_gm


def _kiw_tables(pos):
    c, a, b = _rope_tables(pos, IDX_DIM // 2, IDX_DIM)
    n = pos.shape[0]
    pad = lambda t: jnp.pad(t, ((0, 0), (0, LANES - IDX_DIM)))
    scale = jnp.concatenate([jnp.full((n, N_IDX_HEADS), IDX_WEIGHT_SCALE, F32),
                             jnp.zeros((n, LANES - IDX_DIM - N_IDX_HEADS), F32)], axis=1)
    return jnp.concatenate([c, scale], axis=1), pad(a), pad(b)


def _project_in(x, pos, norm1_g, w_parts, sgu_g, sgu_w, sgu_b, tm, single_step):
    w_qk, w_vqi, w_kiw, w_gu, w_gv = w_parts
    h = _rmsnorm(x, norm1_g, tm, BF16)
    q, kf, kb = _proj_qk(h, w_qk, _rope_tables(pos, HEAD_DIM // 2, LANES), tm)
    vf, vb, qi = _proj_vqi(h, w_vqi, _rope_tables(pos, IDX_DIM // 2, LANES), tm)
    kiw = _proj_kiw(h, w_kiw, _kiw_tables(pos), tm)
    m, vn = _sgu(h, w_gu, w_gv, sgu_g, sgu_w, sgu_b, min(tm, 256), single_step)
    return h, q, kf, kb, vf, vb, qi, kiw, m, vn


def _mix_out(x, h, a, m, w_ga, w_gm, w_a, w_m, w_o, norm2_g, peer_wq, peer_keys, peer_u, peer_v, norm_f_g,
             tm, tt_route, tt_dense):
    mg = _merge(h, a, m, w_ga, w_gm, w_a, w_m, tm, 1024)
    x1, h2 = _outproj(x, mg, w_o, norm2_g, tm)
    s, e1, e2, thr = _peer_route(h2, peer_wq, peer_keys, tt_route)
    po = _peer_dense(h2, s, e1, e2, thr, peer_u, peer_v, tt_dense, 4)
    return _add_rmsnorm(x1, po, norm_f_g, tm)


def kernel(x_prompt, x_sample, cache_k, cache_v, cache_idx_k, page_table, norm1_g, w_in, w_a, w_m, w_o,
           sgu_g, sgu_w, sgu_b, norm2_g, peer_wq, peer_keys, peer_u, peer_v, norm_f_g):
    bsz, seq, d = x_prompt.shape
    db, dec_seq, _ = x_sample.shape
    assert bsz == 1 and dec_seq == 1
    n_pages = page_table.shape[1]
    past = n_pages * PAGE_SIZE
    topk_p = min(TOPK_MAX, seq // 4)
    topk_s = min(TOPK_MAX, (past + dec_seq) // 4)

    w_qk, w_vqi, w_kiw, w_gu, w_gv, w_ga, w_gm = _split_w_in(w_in)
    w_parts = (w_qk, w_vqi, w_kiw, w_gu, w_gv)
    wa_b, wm_b, wo_b = w_a.astype(BF16), w_m.astype(BF16), w_o.astype(BF16)
    wq_b = peer_wq.astype(BF16)
    keys_b = peer_keys.astype(BF16).reshape(2 * PEER_HEADS, PEER_N_KEYS, -1)
    pu_b, pv_b = peer_u.astype(BF16), peer_v.astype(BF16)
    mix_w = (w_ga, w_gm, wa_b, wm_b, wo_b, norm2_g, wq_b, keys_b, pu_b, pv_b, norm_f_g)

    tm = min(512, seq)
    xp = x_prompt.reshape(seq, d)
    pos_p = jnp.arange(seq, dtype=I32)
    h, q, kf, kb, vf, vb, qi, kiw, m, _ = _project_in(xp, pos_p, norm1_g, w_parts, sgu_g, sgu_w, sgu_b, tm, False)
    ki_b = kiw[:, :IDX_DIM].astype(BF16)
    zeros = jnp.zeros_like(ki_b)
    kia = jnp.concatenate([ki_b, zeros], axis=1)
    kib = jnp.concatenate([zeros, ki_b], axis=1)
    a = _dsa_prompt(q, qi, kiw, kb, vb, kia, kib, min(256, seq), topk_p)
    y_p = _mix_out(xp, h, a, m, *mix_w, tm, min(512, seq), min(1024, seq))

    rs = -(-db // LANES) * LANES
    xs = jnp.pad(x_sample.reshape(db, d), ((0, rs - db), (0, 0)))
    pos_s = jnp.full((rs,), past, I32)
    hs, qs, kfs, _, vfs, _, qis, kiws, ms, vns = _project_in(
        xs, pos_s, norm1_g, w_parts, sgu_g, sgu_w, sgu_b, rs, True)
    qi3 = qis[:db].reshape(db, N_IDX_HEADS, IDX_DIM)
    z3 = jnp.zeros_like(qi3)
    lhs = jnp.concatenate([jnp.concatenate([qi3, z3], axis=2), jnp.concatenate([z3, qi3], axis=2)], axis=1)
    wi_s = kiws[:db, IDX_DIM:IDX_DIM + N_IDX_HEADS]
    wcol = jnp.broadcast_to(jnp.tile(wi_s, (1, 2))[:, :, None], (db, 2 * N_IDX_HEADS, LANES))
    ki_s = kiws[:db, :IDX_DIM].astype(BF16)
    kin = jnp.concatenate([ki_s, ki_s], axis=1).reshape(db, 1, LANES)
    cache2 = cache_idx_k.reshape(cache_idx_k.shape[0], PAGE_SIZE // 2, 2 * IDX_DIM)
    sc, scn = _idx_scores(page_table, lhs, wcol, kin, cache2)
    half = past // 2
    pos_cols = jnp.concatenate([2 * jnp.arange(half, dtype=I32), 2 * jnp.arange(half, dtype=I32) + 1]).reshape(1, past)
    idx = _topk_rows(sc.reshape(db, past), scn[:, 0, :1], pos_cols, topk_s, past)
    heads = lambda t: t[:db].reshape(db, N_HEADS, HEAD_DIM)
    a_s = _gather_attend(idx, page_table, heads(qs).astype(F32), heads(kfs), heads(vfs), cache_k, cache_v, past)
    a_s = jnp.pad(a_s.reshape(db, D_ATT), ((0, rs - db), (0, 0))).astype(BF16)
    y_s = _mix_out(xs, hs, a_s, ms, *mix_w, rs, rs, rs)

    return (y_p.reshape(bsz, seq, d), y_s[:db].reshape(db, dec_seq, d),
            kf.reshape(bsz, seq, N_HEADS, HEAD_DIM), vf.reshape(bsz, seq, N_HEADS, HEAD_DIM),
            kiw[:, :IDX_DIM].reshape(bsz, seq, IDX_DIM),
            kfs[:db].reshape(db, dec_seq, N_HEADS, HEAD_DIM), vfs[:db].reshape(db, dec_seq, N_HEADS, HEAD_DIM),
            kiws[:db, :IDX_DIM].reshape(db, dec_seq, IDX_DIM), vns[:db].reshape(db, dec_seq, D_SGU))
```
